```python
import math
import jax
import jax.numpy as jnp
from jax import lax
import numpy as np

D_MODEL = 2048
BATCH = 2
SEQ = 4096
DEPTH = 2
DEC_BATCH = 8
DEC_SEQ = 8
PAST_LEN = 16384
PAGE_SIZE = 128

HEAD_DIM = 128
A_HEADS = 8
A_WIDTH = A_HEADS * HEAD_DIM
IDX_HEADS = 16
IDX_DIM = 64
TOPK_MAX = 256
SSM_GROUP = 16
SSM_WIDTH = 1024
SSM_GROUPS = SSM_WIDTH // SSM_GROUP
SSM_STATE = 64
C_HEADS = 16
C_WIDTH = C_HEADS * HEAD_DIM
REL_BUCKETS = 32
REL_MAX_DIST = 128
D_FF = 5632
N_EXPERTS = 8
TOP_K = 2
D_FF_EXPERT = 7168
MOE_BLOCK = 512
Q_BLOCK = 128
N_EVEN = (DEPTH + 1) // 2
N_ODD = DEPTH // 2
DN_ALPHA = (2.0 * DEPTH) ** 0.25
DN_BETA = (8.0 * DEPTH) ** -0.25
LN_EPS = 1e-5
ATTN_SCALE = HEAD_DIM ** -0.5
IN_E = 3 * A_WIDTH + IDX_HEADS * IDX_DIM + IDX_DIM + IDX_HEADS + SSM_WIDTH
IN_O = 3 * C_WIDTH + C_HEADS

kernel_name = 'hybrid_dsa_s5_fox_decoder_step'


def layer_norm(x, g, b):
    xf = x.astype(jnp.float32)
    mu = jnp.mean(xf, axis=-1, keepdims=True)
    var = jnp.mean(jnp.square(xf - mu), axis=-1, keepdims=True)
    y = (xf - mu) * lax.rsqrt(var + LN_EPS) * g.astype(jnp.float32) + b.astype(jnp.float32)
    return y.astype(x.dtype)


def t5_bucket(dist):
    n = jnp.maximum(dist, 0)
    exact = REL_BUCKETS // 2
    nf = jnp.maximum(n, 1).astype(jnp.float32)
    large = exact + (jnp.log(nf / exact) / math.log(REL_MAX_DIST / exact) * (REL_BUCKETS - exact)).astype(jnp.int32)
    large = jnp.minimum(large, REL_BUCKETS - 1)
    return jnp.where(n < exact, n, large)


def sweep_queries(fn, qpos, *qs):
    t = qpos.shape[0]
    if t <= Q_BLOCK:
        return fn(qpos, *qs)
    nb = t // Q_BLOCK

    def split(a):
        return jnp.swapaxes(a.reshape(a.shape[0], nb, Q_BLOCK, *a.shape[2:]), 0, 1)

    args = (qpos.reshape(nb, Q_BLOCK),) + tuple(split(a) for a in qs)
    out = jnp.swapaxes(lax.map(lambda a: fn(*a), args), 0, 1)
    return out.reshape(out.shape[0], t, *out.shape[3:])


def dsa_block(q, qi, wi, qpos, kidx, kpos, gather_kv, rel_bias, topk):
    causal = kpos[None, :] <= qpos[:, None]
    act = jax.nn.relu(jnp.einsum('bthd,bsd->bths', qi, kidx, preferred_element_type=jnp.float32))
    score = jnp.einsum('bths,bth->bts', act, wi.astype(jnp.float32))
    score = jnp.where(causal[None], score, -jnp.inf)
    _, sel = lax.top_k(score, topk)
    k_sel, v_sel = gather_kv(sel)
    dist = qpos[None, :, None] - sel
    logits = jnp.einsum('bthd,btkhd->bthk', q, k_sel, preferred_element_type=jnp.float32) * ATTN_SCALE
    logits = logits + jnp.moveaxis(rel_bias[t5_bucket(dist)], -1, 2).astype(jnp.float32)
    logits = jnp.where((dist >= 0)[:, :, None, :], logits, -jnp.inf)
    p = jax.nn.softmax(logits, axis=-1).astype(v_sel.dtype)
    return jnp.einsum('bthk,btkhd->bthd', p, v_sel)


def dsa_prompt(q, k, v, qi, ki, wi, rel_bias):
    t = q.shape[1]
    topk = min(TOPK_MAX, t // 4)
    take = jax.vmap(lambda a, i: a[i])
    pos = jnp.arange(t, dtype=jnp.int32)

    def gather_kv(sel):
        return take(k, sel), take(v, sel)

    def fn(qpos, qb, qib, wib):
        return dsa_block(qb, qib, wib, qpos, ki, pos, gather_kv, rel_bias, topk)

    return sweep_queries(fn, pos, q, qi, wi)


def dsa_sample(q, k, v, qi, ki, wi, pool_k, pool_v, pool_ki, page_table, rel_bias):
    b, t = q.shape[:2]
    past = page_table.shape[1] * PAGE_SIZE
    ki_all = jnp.concatenate([pool_ki[page_table].reshape(b, past, IDX_DIM), ki.astype(pool_ki.dtype)], axis=1)
    kpos = jnp.arange(past + t, dtype=jnp.int32)
    topk = min(TOPK_MAX, (past + t) // 4)
    take = jax.vmap(lambda a, i: a[i])

    def gather_kv(sel):
        old = jnp.minimum(sel, past - 1)
        phys = jnp.take_along_axis(page_table, (old // PAGE_SIZE).reshape(b, -1), axis=1).reshape(sel.shape)
        off = old % PAGE_SIZE
        new = jnp.clip(sel - past, 0, t - 1)
        is_new = (sel >= past)[..., None, None]
        k_sel = jnp.where(is_new, take(k, new).astype(pool_k.dtype), pool_k[phys, off])
        v_sel = jnp.where(is_new, take(v, new).astype(pool_v.dtype), pool_v[phys, off])
        return k_sel, v_sel

    def fn(qpos, qb, qib, wib):
        return dsa_block(qb, qib, wib, qpos, ki_all, kpos, gather_kv, rel_bias, topk)

    return sweep_queries(fn, past + jnp.arange(t, dtype=jnp.int32), q, qi, wi)


def s5_discretize(a_re, a_im, log_dt, b_re, b_im):
    lam = lax.complex(a_re.astype(jnp.float32), a_im.astype(jnp.float32))
    dt = jnp.exp(log_dt.astype(jnp.float32))[:, None]
    a_bar = jnp.exp(lam * dt)
    bmat = lax.complex(b_re.astype(jnp.float32), b_im.astype(jnp.float32))
    b_bar = ((a_bar - 1.0) / lam)[..., None] * bmat
    return a_bar, b_bar


def s5_scan(u, h0, a_bar, b_bar, c_re, c_im, d):
    uf = u.astype(jnp.float32)
    bu = jnp.einsum('btgi,gpi->btgp', uf.astype(jnp.complex64), b_bar)
    a = jnp.broadcast_to(a_bar, bu.shape)

    def combine(x, y):
        return (y[0] * x[0], y[0] * x[1] + y[1])

    a_cum, h = lax.associative_scan(combine, (a, bu), axis=1)
    if h0 is not None:
        h = h + a_cum * h0[:, None]
    cmat = lax.complex(c_re.astype(jnp.float32), c_im.astype(jnp.float32))
    y = jnp.einsum('gip,btgp->btgi', cmat, h).real + d.astype(jnp.float32) * uf
    return y, h[:, -1]


def fox_block(q, fq, qpos, k, v, fk, kpos):
    s = jnp.einsum('bthd,bshd->bhts', q, k, preferred_element_type=jnp.float32) * ATTN_SCALE
    s = s + jnp.swapaxes(fq, 1, 2)[..., None] - jnp.swapaxes(fk, 1, 2)[:, :, None, :]
    s = jnp.where(kpos[None, :] <= qpos[:, None], s, -jnp.inf)
    p = jax.nn.softmax(s, axis=-1).astype(v.dtype)
    return jnp.einsum('bhts,bshd->bthd', p, v)


def fox_prompt(q, k, v, logf):
    t = q.shape[1]
    cum = jnp.cumsum(logf, axis=1)
    pos = jnp.arange(t, dtype=jnp.int32)

    def fn(qpos, qb, fqb):
        return fox_block(qb, fqb, qpos, k, v, cum, pos)

    return sweep_queries(fn, pos, q, cum)


def fox_sample(q, k, v, logf, pool_k, pool_v, pool_lf, page_table):
    b, t = q.shape[:2]
    past = page_table.shape[1] * PAGE_SIZE
    k_all = jnp.concatenate([pool_k[page_table].reshape(b, past, C_HEADS, HEAD_DIM), k.astype(pool_k.dtype)], axis=1)
    v_all = jnp.concatenate([pool_v[page_table].reshape(b, past, C_HEADS, HEAD_DIM), v.astype(pool_v.dtype)], axis=1)
    lf_all = jnp.concatenate([pool_lf[page_table].reshape(b, past, C_HEADS).astype(jnp.float32), logf], axis=1)
    cum = jnp.cumsum(lf_all, axis=1)
    kpos = jnp.arange(past + t, dtype=jnp.int32)

    def fn(qpos, qb, fqb):
        return fox_block(qb, fqb, qpos, k_all, v_all, cum, kpos)

    return sweep_queries(fn, past + jnp.arange(t, dtype=jnp.int32), q, cum[:, past:])


def dense_swiglu(h, wg, wu, wd):
    return (jax.nn.silu(h @ wg) * (h @ wu)) @ wd


def moe_swiglu(h, router_w, router_b, wg, wu, wd):
    b, t, d = h.shape
    n = b * t
    xf = h.reshape(n, d)
    logits = (xf @ router_w).astype(jnp.float32) + router_b.astype(jnp.float32)
    top_val, top_idx = lax.top_k(logits, TOP_K)
    gates = jax.nn.softmax(top_val, axis=-1)
    e_flat = top_idx.reshape(-1)
    tok_flat = jnp.repeat(jnp.arange(n, dtype=jnp.int32), TOP_K)
    g_flat = gates.reshape(-1)
    n_assign = n * TOP_K
    blk = max(8, min(MOE_BLOCK, n_assign // N_EXPERTS))
    n_blocks = (n_assign + N_EXPERTS * (blk - 1) + blk - 1) // blk
    rows = n_blocks * blk
    order = jnp.argsort(e_flat)
    e_sorted = e_flat[order]
    counts = jnp.bincount(e_flat, length=N_EXPERTS)
    padded = (counts + blk - 1) // blk * blk
    pad_end = jnp.cumsum(padded)
    pad_start = pad_end - padded
    start = jnp.cumsum(counts) - counts
    dest = pad_start[e_sorted] + jnp.arange(n_assign) - start[e_sorted]
    row_tok = jnp.full((rows,), n, jnp.int32).at[dest].set(tok_flat[order])
    row_gate = jnp.zeros((rows,), jnp.float32).at[dest].set(g_flat[order])
    blk_exp = jnp.minimum(jnp.searchsorted(pad_end, jnp.arange(n_blocks) * blk, side='right'), N_EXPERTS - 1)
    x_pad = jnp.concatenate([xf, jnp.zeros((1, d), xf.dtype)], axis=0)
    xs = x_pad[row_tok].reshape(n_blocks, blk, d)

    def expert_block(args):
        xb, e = args
        return (jax.nn.silu(xb @ wg[e]) * (xb @ wu[e])) @ wd[e]

    ys = lax.map(expert_block, (xs, blk_exp)).reshape(rows, d)
    ys = ys * row_gate[:, None].astype(ys.dtype)
    out = jnp.zeros((n + 1, d), ys.dtype).at[row_tok].add(ys)[:n]
    return out.reshape(b, t, d)


def mixer_even(h, le, p, past):
    b, t, _ = h.shape
    proj = h @ p['w_in_e'][le]
    sizes = (A_WIDTH, A_WIDTH, A_WIDTH, IDX_HEADS * IDX_DIM, IDX_DIM, IDX_HEADS, SSM_WIDTH)
    cuts = [sum(sizes[: i + 1]) for i in range(len(sizes) - 1)]
    q, k, v, qi, ki, wi, u = jnp.split(proj, cuts, axis=-1)
    q = q.reshape(b, t, A_HEADS, HEAD_DIM)
    k = k.reshape(b, t, A_HEADS, HEAD_DIM)
    v = v.reshape(b, t, A_HEADS, HEAD_DIM)
    qi = qi.reshape(b, t, IDX_HEADS, IDX_DIM)
    if past is None:
        att = dsa_prompt(q, k, v, qi, ki, wi, p['rel_bias'])
        h0 = None
    else:
        att = dsa_sample(q, k, v, qi, ki, wi, past['a_k'][le], past['a_v'][le], past['a_kidx'][le],
                         past['page_table'], p['rel_bias'])
        h0 = lax.complex(past['ssm_re'][le].astype(jnp.float32), past['ssm_im'][le].astype(jnp.float32))
    a_bar, b_bar = s5_discretize(p['ssm_a_re'][le], p['ssm_a_im'][le], p['ssm_log_dt'][le],
                                 p['ssm_b_re'][le], p['ssm_b_im'][le])
    y, h_last = s5_scan(u.reshape(b, t, SSM_GROUPS, SSM_GROUP), h0, a_bar, b_bar,
                        p['ssm_c_re'][le], p['ssm_c_im'][le], p['ssm_d'][le])
    z = jax.nn.gelu(y.reshape(b, t, SSM_WIDTH))
    ssm = z * jax.nn.sigmoid(z @ p['glu_w'][le].astype(jnp.float32) + p['glu_b'][le].astype(jnp.float32))
    mixed = jnp.concatenate([att.reshape(b, t, A_WIDTH).astype(h.dtype), ssm.astype(h.dtype)], axis=-1)
    return mixed @ p['w_out_e'][le], (k, v, ki, h_last.real, h_last.imag)


def mixer_odd(h, lo, p, past):
    b, t, _ = h.shape
    proj = h @ p['w_in_o'][lo]
    q = proj[..., :C_WIDTH].reshape(b, t, C_HEADS, HEAD_DIM)
    k = proj[..., C_WIDTH:2 * C_WIDTH].reshape(b, t, C_HEADS, HEAD_DIM)
    v = proj[..., 2 * C_WIDTH:3 * C_WIDTH].reshape(b, t, C_HEADS, HEAD_DIM)
    logf = jax.nn.log_sigmoid(proj[..., 3 * C_WIDTH:].astype(jnp.float32) + p['fox_b_f'][lo].astype(jnp.float32))
    if past is None:
        out = fox_prompt(q, k, v, logf)
    else:
        out = fox_sample(q, k, v, logf, past['c_k'][lo], past['c_v'][lo], past['c_logf'][lo], past['page_table'])
    return out.reshape(b, t, C_WIDTH).astype(h.dtype) @ p['w_out_o'][lo], (k, v, logf)


def trunk(x, c, p, past):
    b, t, d = x.shape
    new_e, new_o = [], []
    for l in range(DEPTH):
        li = l // 2
        mod = (jax.nn.silu(c) @ p['ada_w'][l] + p['ada_b'][l]).reshape(b, 6, 1, d)
        h = x * (1.0 + mod[:, 1]) + mod[:, 0]
        if l % 2 == 0:
            y, st = mixer_even(h, li, p, past)
            new_e.append(st)
        else:
            y, st = mixer_odd(h, li, p, past)
            new_o.append(st)
        x = layer_norm(DN_ALPHA * x + (1.0 + mod[:, 2]) * y, p['ln_g'][l, 0], p['ln_b'][l, 0])
        h = x * (1.0 + mod[:, 4]) + mod[:, 3]
        if l % 2 == 0:
            y = dense_swiglu(h, p['ffn_w_gate'][li], p['ffn_w_up'][li], p['ffn_w_down'][li])
        else:
            y = moe_swiglu(h, p['router_w'][li], p['router_b'][li], p['moe_w_gate'][li],
                           p['moe_w_up'][li], p['moe_w_down'][li])
        x = layer_norm(DN_ALPHA * x + (1.0 + mod[:, 5]) * y, p['ln_g'][l, 1], p['ln_b'][l, 1])
    st_e = [jnp.stack(z) for z in zip(*new_e)]
    st_o = [jnp.stack(z) for z in zip(*new_o)]
    return x, st_e, st_o


def setup_inputs(seed: int = 0) -> dict:
    key = jax.random.key(seed)
    keys = iter(jax.random.split(key, 48))

    def nrm(shape, scale):
        return jax.random.normal(next(keys), shape, jnp.float32) * scale

    d = D_MODEL
    n_pages = PAST_LEN // PAGE_SIZE
    n_pool = (5 * DEC_BATCH * n_pages + 3) // 4
    page_table = jax.random.permutation(next(keys), n_pool)[: DEC_BATCH * n_pages]
    page_table = page_table.reshape(DEC_BATCH, n_pages).astype(jnp.int32)
    log_dt = jax.random.uniform(next(keys), (N_EVEN, SSM_GROUPS), jnp.float32, math.log(1e-3), math.log(1e-1))
    return {
        'x_prompt': nrm((BATCH, SEQ, d), 1.0),
        'x_sample': nrm((DEC_BATCH, DEC_SEQ, d), 1.0),
        'cache_a_k': nrm((N_EVEN, n_pool, PAGE_SIZE, A_HEADS, HEAD_DIM), 1.0),
        'cache_a_v': nrm((N_EVEN, n_pool, PAGE_SIZE, A_HEADS, HEAD_DIM), 1.0),
        'cache_a_kidx': nrm((N_EVEN, n_pool, PAGE_SIZE, IDX_DIM), 1.0),
        'state_ssm_re': nrm((N_EVEN, DEC_BATCH, SSM_GROUPS, SSM_STATE), 0.1),
        'state_ssm_im': nrm((N_EVEN, DEC_BATCH, SSM_GROUPS, SSM_STATE), 0.1),
        'cache_c_k': nrm((N_ODD, n_pool, PAGE_SIZE, C_HEADS, HEAD_DIM), 1.0),
        'cache_c_v': nrm((N_ODD, n_pool, PAGE_SIZE, C_HEADS, HEAD_DIM), 1.0),
        'cache_c_logf': jax.nn.log_sigmoid(nrm((N_ODD, n_pool, PAGE_SIZE, C_HEADS), 1.0) + 2.0),
        'page_table': page_table,
        'c_prompt': nrm((BATCH, d), 1.0),
        'c_sample': nrm((DEC_BATCH, d), 1.0),
        'rel_bias': nrm((REL_BUCKETS, A_HEADS), 0.5),
        'ada_w': nrm((DEPTH, d, 6 * d), 0.5 * d ** -0.5),
        'ada_b': nrm((DEPTH, 6 * d), 0.02),
        'ln_g': 1.0 + nrm((DEPTH, 2, d), 0.02),
        'ln_b': nrm((DEPTH, 2, d), 0.02),
        'w_in_e': nrm((N_EVEN, d, IN_E), d ** -0.5),
        'ssm_a_re': -0.5 + nrm((N_EVEN, SSM_GROUPS, SSM_STATE), 0.01),
        'ssm_a_im': math.pi * jnp.arange(SSM_STATE, dtype=jnp.float32) + nrm((N_EVEN, SSM_GROUPS, SSM_STATE), 0.01),
        'ssm_log_dt': log_dt,
        'ssm_b_re': nrm((N_EVEN, SSM_GROUPS, SSM_STATE, SSM_GROUP), (2 * SSM_GROUP) ** -0.5),
        'ssm_b_im': nrm((N_EVEN, SSM_GROUPS, SSM_STATE, SSM_GROUP), (2 * SSM_GROUP) ** -0.5),
        'ssm_c_re': nrm((N_EVEN, SSM_GROUPS, SSM_GROUP, SSM_STATE), SSM_STATE ** -0.5),
        'ssm_c_im': nrm((N_EVEN, SSM_GROUPS, SSM_GROUP, SSM_STATE), SSM_STATE ** -0.5),
        'ssm_d': nrm((N_EVEN, SSM_GROUPS, SSM_GROUP), 1.0),
        'glu_w': nrm((N_EVEN, SSM_WIDTH, SSM_WIDTH), SSM_WIDTH ** -0.5),
        'glu_b': nrm((N_EVEN, SSM_WIDTH), 0.02),
        'w_out_e': nrm((N_EVEN, A_WIDTH + SSM_WIDTH, d), DN_BETA * (A_WIDTH + SSM_WIDTH) ** -0.5),
        'ffn_w_gate': nrm((N_EVEN, d, D_FF), d ** -0.5),
        'ffn_w_up': nrm((N_EVEN, d, D_FF), d ** -0.5),
        'ffn_w_down': nrm((N_EVEN, D_FF, d), DN_BETA * D_FF ** -0.5),
        'w_in_o': nrm((N_ODD, d, IN_O), d ** -0.5),
        'fox_b_f': 1.0 + nrm((N_ODD, C_HEADS), 0.5),
        'w_out_o': nrm((N_ODD, C_WIDTH, d), DN_BETA * C_WIDTH ** -0.5),
        'router_w': nrm((N_ODD, d, N_EXPERTS), d ** -0.5),
        'router_b': nrm((N_ODD, N_EXPERTS), 0.01),
        'moe_w_gate': nrm((N_ODD, N_EXPERTS, d, D_FF_EXPERT), d ** -0.5),
        'moe_w_up': nrm((N_ODD, N_EXPERTS, d, D_FF_EXPERT), d ** -0.5),
        'moe_w_down': nrm((N_ODD, N_EXPERTS, D_FF_EXPERT, d), DN_BETA * D_FF_EXPERT ** -0.5),
    }


def reference(x_prompt, x_sample, cache_a_k, cache_a_v, cache_a_kidx, state_ssm_re, state_ssm_im,
              cache_c_k, cache_c_v, cache_c_logf, page_table, c_prompt, c_sample,
              rel_bias, ada_w, ada_b, ln_g, ln_b, w_in_e, ssm_a_re, ssm_a_im, ssm_log_dt,
              ssm_b_re, ssm_b_im, ssm_c_re, ssm_c_im, ssm_d, glu_w, glu_b, w_out_e,
              ffn_w_gate, ffn_w_up, ffn_w_down, w_in_o, fox_b_f, w_out_o,
              router_w, router_b, moe_w_gate, moe_w_up, moe_w_down):
    p = dict(rel_bias=rel_bias, ada_w=ada_w, ada_b=ada_b, ln_g=ln_g, ln_b=ln_b, w_in_e=w_in_e,
             ssm_a_re=ssm_a_re, ssm_a_im=ssm_a_im, ssm_log_dt=ssm_log_dt, ssm_b_re=ssm_b_re,
             ssm_b_im=ssm_b_im, ssm_c_re=ssm_c_re, ssm_c_im=ssm_c_im, ssm_d=ssm_d, glu_w=glu_w,
             glu_b=glu_b, w_out_e=w_out_e, ffn_w_gate=ffn_w_gate, ffn_w_up=ffn_w_up,
             ffn_w_down=ffn_w_down, w_in_o=w_in_o, fox_b_f=fox_b_f, w_out_o=w_out_o,
             router_w=router_w, router_b=router_b, moe_w_gate=moe_w_gate, moe_w_up=moe_w_up,
             moe_w_down=moe_w_down)
    past = dict(a_k=cache_a_k, a_v=cache_a_v, a_kidx=cache_a_kidx, ssm_re=state_ssm_re,
                ssm_im=state_ssm_im, c_k=cache_c_k, c_v=cache_c_v, c_logf=cache_c_logf,
                page_table=page_table)
    y_prompt, st_e_p, st_o_p = trunk(x_prompt, c_prompt, p, None)
    y_sample, st_e_s, st_o_s = trunk(x_sample, c_sample, p, past)
    pa_k, pa_v, pa_ki, ps_re, ps_im = st_e_p
    sa_k, sa_v, sa_ki, ss_re, ss_im = st_e_s
    pc_k, pc_v, pc_lf = st_o_p
    sc_k, sc_v, sc_lf = st_o_s
    return (y_prompt, y_sample, pa_k, sa_k, pa_v, sa_v, pa_ki, sa_ki, ps_re, ss_re, ps_im, ss_im,
            pc_k, sc_k, pc_v, sc_v, pc_lf, sc_lf)
```

```python
import functools
import math

import jax
import jax.numpy as jnp
from jax import lax
from jax.experimental import pallas as pl
from jax.experimental.pallas import tpu as pltpu

F32 = jnp.float32
BF16 = jnp.bfloat16
I32 = jnp.int32

D_MODEL = 2048
BATCH = 2
SEQ = 4096
DEPTH = 2
DEC_BATCH = 8
DEC_SEQ = 8
PAST_LEN = 16384
PAGE_SIZE = 128
N_PAGES = PAST_LEN // PAGE_SIZE
HEAD_DIM = 128
A_HEADS = 8
A_WIDTH = A_HEADS * HEAD_DIM
IDX_HEADS = 16
IDX_DIM = 64
TOPK_MAX = 256
SSM_GROUP = 16
SSM_WIDTH = 1024
SSM_GROUPS = SSM_WIDTH // SSM_GROUP
SSM_STATE = 64
C_HEADS = 16
C_WIDTH = C_HEADS * HEAD_DIM
REL_BUCKETS = 32
REL_MAX_DIST = 128
D_FF = 5632
N_EXPERTS = 8
TOP_K = 2
D_FF_EXPERT = 7168
DN_ALPHA = (2.0 * DEPTH) ** 0.25
LN_EPS = 1e-5
ATTN_SCALE = HEAD_DIM ** -0.5

N_PROMPT = BATCH * SEQ
N_SAMPLE = DEC_BATCH * DEC_SEQ
N_ROWS = N_PROMPT + N_SAMPLE
N_COND = 16

ROW_TILE = 64
MM_TILE = 688
MOE_BLOCK = 512
Q_TILE = 256
K_CHUNK = 256
SSM_CHUNK = 16
NEG = -1e30
VMEM_LIMIT = 56 * 1024 * 1024


def _cparams(n_axes, vmem=VMEM_LIMIT):
    return pltpu.CompilerParams(dimension_semantics=("arbitrary",) * n_axes, vmem_limit_bytes=vmem)


def _ada_kernel(c_ref, w_ref, b_ref, o_ref):
    c = c_ref[...]
    s = (c * jax.nn.sigmoid(c)).astype(BF16)
    o_ref[0] = jnp.dot(s, w_ref[0].astype(BF16), preferred_element_type=F32) + b_ref[0]


def _ada_mod(c_all, ada_w, ada_b):
    tn = 1024
    nt = 6 * D_MODEL // tn
    return pl.pallas_call(
        _ada_kernel,
        out_shape=jax.ShapeDtypeStruct((DEPTH, N_COND, 6 * D_MODEL), F32),
        grid=(DEPTH, nt),
        in_specs=[
            pl.BlockSpec((N_COND, D_MODEL), lambda l, j: (0, 0)),
            pl.BlockSpec((1, D_MODEL, tn), lambda l, j: (l, 0, j)),
            pl.BlockSpec((1, 1, tn), lambda l, j: (l, 0, j)),
        ],
        out_specs=pl.BlockSpec((1, N_COND, tn), lambda l, j: (l, 0, j)),
        compiler_params=_cparams(2),
        name="ada_mod",
    )(c_all, ada_w, ada_b.reshape(DEPTH, 1, 6 * D_MODEL))


_PROMPT_TILES = N_PROMPT // ROW_TILE
_TILES_PER_SEQ = SEQ // ROW_TILE


def _row_mod(tab_ref, tabrow_ref, k):
    is_sample = pl.program_id(0) >= _PROMPT_TILES
    return jnp.where(is_sample, tabrow_ref[k], tab_ref[0, k:k + 1, :])


def _modulate_kernel(x_ref, tab_ref, tabrow_ref, h_ref):
    shift = _row_mod(tab_ref, tabrow_ref, 0)
    scale = _row_mod(tab_ref, tabrow_ref, 1)
    h_ref[...] = (x_ref[...] * (1.0 + scale) + shift).astype(h_ref.dtype)


def _ln_kernel(x_ref, y_ref, tab_ref, tabrow_ref, g_ref, b_ref, xn_ref, *h_refs):
    gate = _row_mod(tab_ref, tabrow_ref, 0)
    v = DN_ALPHA * x_ref[...] + (1.0 + gate) * y_ref[...]
    mu = jnp.mean(v, axis=-1, keepdims=True)
    d = v - mu
    var = jnp.mean(d * d, axis=-1, keepdims=True)
    xn = d * lax.rsqrt(var + LN_EPS) * g_ref[...] + b_ref[...]
    xn_ref[...] = xn
    if h_refs:
        shift = _row_mod(tab_ref, tabrow_ref, 1)
        scale = _row_mod(tab_ref, tabrow_ref, 2)
        h_refs[0][...] = (xn * (1.0 + scale) + shift).astype(h_refs[0].dtype)


def _mod_tables(vectors):
    tab = jnp.stack(vectors, axis=1)
    rows = jnp.stack([jnp.repeat(v[BATCH:BATCH + DEC_BATCH], DEC_SEQ, axis=0) for v in vectors], axis=0)
    return tab, rows


def _row_specs(n_vec):
    row = pl.BlockSpec((ROW_TILE, D_MODEL), lambda i: (i, 0))
    tab = pl.BlockSpec((1, n_vec, D_MODEL), lambda i: (jnp.minimum(i // _TILES_PER_SEQ, BATCH - 1), 0, 0))
    tabrow = pl.BlockSpec((n_vec, ROW_TILE, D_MODEL), lambda i: (0, 0, 0))
    return row, tab, tabrow


def _modulate(x, shift, scale):
    tab, tabrow = _mod_tables([shift, scale])
    row, tabs, tabrows = _row_specs(2)
    return pl.pallas_call(
        _modulate_kernel,
        out_shape=jax.ShapeDtypeStruct((N_ROWS, D_MODEL), BF16),
        grid=(N_ROWS // ROW_TILE,),
        in_specs=[row, tabs, tabrows],
        out_specs=row,
        compiler_params=_cparams(1),
        name="modulate",
    )(x, tab, tabrow)


def _ln_mod(x, y, gate, g, b, shift=None, scale=None, h_dtype=BF16):
    has_h = shift is not None
    vecs = [gate, shift, scale] if has_h else [gate]
    tab, tabrow = _mod_tables(vecs)
    row, tabs, tabrows = _row_specs(len(vecs))
    vec = pl.BlockSpec((1, D_MODEL), lambda i: (0, 0))
    out_shape = [jax.ShapeDtypeStruct((N_ROWS, D_MODEL), F32)]
    out_specs = [row]
    if has_h:
        out_shape.append(jax.ShapeDtypeStruct((N_ROWS, D_MODEL), h_dtype))
        out_specs.append(row)
    outs = pl.pallas_call(
        _ln_kernel,
        out_shape=out_shape,
        grid=(N_ROWS // ROW_TILE,),
        in_specs=[row, row, tabs, tabrows, vec, vec],
        out_specs=out_specs,
        compiler_params=_cparams(1),
        name="ln_mod",
    )(x, y, tab, tabrow, g.reshape(1, D_MODEL), b.reshape(1, D_MODEL))
    return outs if has_h else (outs[0], None)


def _weight_changed(be_ref):
    i = pl.program_id(1)
    prev = be_ref[jnp.maximum(i - 1, 0)]
    return jnp.logical_or(i == 0, be_ref[i] != prev)


def _mm_kernel(be_ref, x_ref, w_ref, o_ref, wb_ref):
    @pl.when(_weight_changed(be_ref))
    def _():
        wb_ref[...] = w_ref[0].astype(BF16)

    o_ref[...] = jnp.dot(x_ref[...], wb_ref[...], preferred_element_type=F32).astype(o_ref.dtype)


def _mm_gated_kernel(be_ref, x_ref, wg_ref, wu_ref, o_ref, wgb_ref, wub_ref):
    @pl.when(_weight_changed(be_ref))
    def _():
        wgb_ref[...] = wg_ref[0].astype(BF16)
        wub_ref[...] = wu_ref[0].astype(BF16)

    x = x_ref[...]
    a = jnp.dot(x, wgb_ref[...], preferred_element_type=F32)
    u = jnp.dot(x, wub_ref[...], preferred_element_type=F32)
    o_ref[...] = (a * jax.nn.sigmoid(a) * u).astype(o_ref.dtype)


def _grouped_mm(x, weights, block_expert, tm, tn, out_dtype):
    m, k = x.shape
    n = weights[0].shape[2]
    assert m % tm == 0 and n % tn == 0
    gated = len(weights) == 2
    w_spec = pl.BlockSpec((1, k, tn), lambda j, i, be: (be[i], 0, j))
    grid_spec = pltpu.PrefetchScalarGridSpec(
        num_scalar_prefetch=1,
        grid=(n // tn, m // tm),
        in_specs=[pl.BlockSpec((tm, k), lambda j, i, be: (i, 0))] + [w_spec] * len(weights),
        out_specs=pl.BlockSpec((tm, tn), lambda j, i, be: (i, j)),
        scratch_shapes=[pltpu.VMEM((k, tn), BF16)] * len(weights),
    )
    return pl.pallas_call(
        _mm_gated_kernel if gated else _mm_kernel,
        out_shape=jax.ShapeDtypeStruct((m, n), out_dtype),
        grid_spec=grid_spec,
        compiler_params=_cparams(2),
        name="mm_gated" if gated else "mm",
    )(block_expert, x, *weights)


def _dense_mm(x, w, tn, out_dtype=F32):
    be = jnp.zeros((x.shape[0] // MM_TILE,), I32)
    return _grouped_mm(x, [w[None]], be, MM_TILE, tn, out_dtype)


def _dense_mm_gated(x, wg, wu, tn):
    be = jnp.zeros((x.shape[0] // MM_TILE,), I32)
    return _grouped_mm(x, [wg[None], wu[None]], be, MM_TILE, tn, BF16)


def _sort_key(score):
    bits = pltpu.bitcast(score + 0.0, I32)
    return bits ^ ((bits >> 31) & 0x7FFFFFFF)


def _kth_largest_key(count_ge, k, rows):
    int_min = jnp.full((rows, 1), -2 ** 31, I32)
    t0 = jnp.where(count_ge(jnp.zeros((rows, 1), I32)) >= k, 0, int_min)

    def body(it, t):
        cand = t + jnp.left_shift(jnp.int32(1), 30 - it)
        return jnp.where(count_ge(cand) >= k, cand, t)

    return lax.fori_loop(0, 31, body, t0)


def _tie_cut(count_eq_below, need, rows, n_bits):
    def body(it, p):
        cand = p + jnp.left_shift(jnp.int32(1), n_bits - 1 - it)
        return jnp.where(count_eq_below(cand) < need, cand, p)

    return lax.fori_loop(0, n_bits, body, jnp.zeros((rows, 1), I32))


def _dsa_index_prompt_kernel(qi_ref, wi_ref, kit_ref, mb_ref, key_ref, cut_ref):
    qb = pl.program_id(1)
    n_chunks = qb + 1
    rows = Q_TILE
    wi = wi_ref[0]
    row_pos = qb * Q_TILE + lax.broadcasted_iota(I32, (rows, K_CHUNK), 0)
    col_iota = lax.broadcasted_iota(I32, (rows, K_CHUNK), 1)

    def score_chunk(c, _):
        off = pl.multiple_of(c * K_CHUNK, K_CHUNK)
        kt = kit_ref[0, :, pl.ds(off, K_CHUNK)]
        acc = jnp.zeros((rows, K_CHUNK), F32)
        for h in range(IDX_HEADS):
            a = jnp.dot(qi_ref[0, :, h * IDX_DIM:(h + 1) * IDX_DIM], kt, preferred_element_type=F32)
            acc = acc + jnp.maximum(a, 0.0) * wi[:, h:h + 1]
        causal = (off + col_iota) <= row_pos
        key_ref[:, pl.ds(off, K_CHUNK)] = _sort_key(jnp.where(causal, acc, -jnp.inf))
        return 0

    lax.fori_loop(0, n_chunks, score_chunk, 0)

    def count(pred):
        def body(c, tot):
            off = pl.multiple_of(c * K_CHUNK, K_CHUNK)
            hit = pred(key_ref[:, pl.ds(off, K_CHUNK)], off + col_iota)
            return tot + jnp.sum(jnp.where(hit, 1.0, 0.0), axis=1, keepdims=True)

        return lax.fori_loop(0, n_chunks, body, jnp.zeros((rows, 1), F32))

    k = float(TOPK_MAX)
    thr = _kth_largest_key(lambda t: count(lambda key, idx: key >= t), k, rows)
    n_ge = count(lambda key, idx: key >= thr)
    n_gt = count(lambda key, idx: key > thr)
    need = k - n_gt
    cut_ref[...] = jnp.full((rows, 1), SEQ, I32)

    @pl.when(jnp.max(n_ge) > k)
    def _():
        cut_ref[...] = _tie_cut(lambda p: count(lambda key, idx: (key == thr) & (idx < p)), need, rows, 12)

    cut = cut_ref[...]
    mb_ref[0] = jnp.full((rows, SEQ), NEG, BF16)

    def write_chunk(c, _):
        off = pl.multiple_of(c * K_CHUNK, K_CHUNK)
        key = key_ref[:, pl.ds(off, K_CHUNK)]
        idx = off + col_iota
        keep = ((key > thr) | ((key == thr) & (idx <= cut))) & (idx <= row_pos)
        mb_ref[0, :, pl.ds(off, K_CHUNK)] = jnp.where(keep, 0.0, NEG).astype(BF16)
        return 0

    lax.fori_loop(0, n_chunks, write_chunk, 0)


def _dsa_index_prompt(qi, wi, kit):
    nq = SEQ // Q_TILE
    return pl.pallas_call(
        _dsa_index_prompt_kernel,
        out_shape=jax.ShapeDtypeStruct((BATCH, SEQ, SEQ), BF16),
        grid=(BATCH, nq),
        in_specs=[
            pl.BlockSpec((1, Q_TILE, IDX_HEADS * IDX_DIM), lambda b, q: (b, q, 0)),
            pl.BlockSpec((1, Q_TILE, IDX_HEADS), lambda b, q: (b, q, 0)),
            pl.BlockSpec((1, IDX_DIM, SEQ), lambda b, q: (b, 0, 0)),
        ],
        out_specs=pl.BlockSpec((1, Q_TILE, SEQ), lambda b, q: (b, q, 0)),
        scratch_shapes=[pltpu.VMEM((Q_TILE, SEQ), I32), pltpu.VMEM((Q_TILE, 1), I32)],
        compiler_params=_cparams(2),
        name="dsa_index_prompt",
    )(qi, wi, kit)


def _softmax_step(s, v, m, l, acc):
    m_new = jnp.maximum(m, jnp.max(s, axis=1, keepdims=True))
    alpha = jnp.exp(m - m_new)
    p = jnp.exp(s - m_new)
    l_new = alpha * l + jnp.sum(p, axis=1, keepdims=True)
    acc_new = alpha * acc + jnp.dot(p.astype(BF16), v, preferred_element_type=F32)
    return m_new, l_new, acc_new


def _dot_nt(a, b):
    return lax.dot_general(a, b, (((1,), (1,)), ((), ())), preferred_element_type=F32)


def _dsa_attn_prompt_kernel(far_ref, q_ref, k_ref, v_ref, mb_ref, tb_ref, o_ref):
    qb = pl.program_id(1)
    rows = Q_TILE
    for h in range(A_HEADS):
        lanes = slice(h * HEAD_DIM, (h + 1) * HEAD_DIM)
        q = q_ref[0, :, lanes]

        def logits(c):
            off = pl.multiple_of(c * K_CHUNK, K_CHUNK)
            s = _dot_nt(q, k_ref[0, pl.ds(off, K_CHUNK), lanes]) * ATTN_SCALE
            return s + mb_ref[0, :, pl.ds(off, K_CHUNK)].astype(F32), v_ref[0, pl.ds(off, K_CHUNK), lanes]

        def far_body(c, carry):
            s, v = logits(c)
            return _softmax_step(s + far_ref[h], v, *carry)

        carry = (jnp.full((rows, 1), NEG, F32), jnp.zeros((rows, 1), F32), jnp.zeros((rows, HEAD_DIM), F32))
        carry = lax.fori_loop(0, jnp.maximum(qb - 1, 0), far_body, carry)
        s, v = logits(jnp.maximum(qb - 1, 0))
        s = s + jnp.where(qb >= 1, tb_ref[1, h], NEG)
        carry = _softmax_step(s, v, *carry)
        s, v = logits(qb)
        m, l, acc = _softmax_step(s + tb_ref[0, h], v, *carry)
        o_ref[0, :, lanes] = (acc / l).astype(o_ref.dtype)


def _dsa_attn_prompt(far_bias, q, k, v, maskbias, tile_bias):
    nq = SEQ // Q_TILE
    grid_spec = pltpu.PrefetchScalarGridSpec(
        num_scalar_prefetch=0,
        grid=(BATCH, nq),
        in_specs=[
            pl.BlockSpec(memory_space=pltpu.SMEM),
            pl.BlockSpec((1, Q_TILE, A_WIDTH), lambda b, i: (b, i, 0)),
            pl.BlockSpec((1, SEQ, A_WIDTH), lambda b, i: (b, 0, 0)),
            pl.BlockSpec((1, SEQ, A_WIDTH), lambda b, i: (b, 0, 0)),
            pl.BlockSpec((1, Q_TILE, SEQ), lambda b, i: (b, i, 0)),
            pl.BlockSpec((2, A_HEADS, Q_TILE, K_CHUNK), lambda b, i: (0, 0, 0, 0)),
        ],
        out_specs=pl.BlockSpec((1, Q_TILE, A_WIDTH), lambda b, i: (b, i, 0)),
    )
    return pl.pallas_call(
        _dsa_attn_prompt_kernel,
        out_shape=jax.ShapeDtypeStruct((BATCH, SEQ, A_WIDTH), BF16),
        grid_spec=grid_spec,
        compiler_params=_cparams(2),
        name="dsa_attn_prompt",
    )(far_bias, q, k, v, maskbias, tile_bias)


_S_KEYS = PAST_LEN + PAGE_SIZE
_S_TOPK = min(TOPK_MAX, (PAST_LEN + DEC_SEQ) // 4)


def _dsa_index_sample_kernel(pt_ref, qi_ref, wi_ref, kpage_ref, knew_ref, mb_ref, key_ref):
    page = pl.program_id(1)
    t_rows = DEC_SEQ
    wi = wi_ref[0]
    qi = qi_ref[0]

    def scores(keys_bf16):
        a = _dot_nt(qi, keys_bf16)
        acc = jnp.zeros((t_rows, keys_bf16.shape[0]), F32)
        for h in range(IDX_HEADS):
            acc = acc + jnp.maximum(a[h * t_rows:(h + 1) * t_rows], 0.0) * wi[:, h:h + 1]
        return acc

    off = pl.multiple_of(page * PAGE_SIZE, PAGE_SIZE)
    key_ref[:, pl.ds(off, PAGE_SIZE)] = _sort_key(scores(kpage_ref[0].astype(BF16)))

    @pl.when(page == N_PAGES - 1)
    def _():
        s_new = scores(knew_ref[0])
        t_idx = lax.broadcasted_iota(I32, (t_rows, PAGE_SIZE), 0)
        s_idx = lax.broadcasted_iota(I32, (t_rows, PAGE_SIZE), 1)
        key_ref[:, PAST_LEN:] = _sort_key(jnp.where(s_idx <= t_idx, s_new, -jnp.inf))

        keys = key_ref[...]
        idx = lax.broadcasted_iota(I32, (t_rows, _S_KEYS), 1)

        def count(hit):
            return jnp.sum(jnp.where(hit, 1.0, 0.0), axis=1, keepdims=True)

        k = float(_S_TOPK)
        thr = _kth_largest_key(lambda t: count(keys >= t), k, t_rows)
        need = k - count(keys > thr)
        cut = _tie_cut(lambda p: count((keys == thr) & (idx < p)), need, t_rows, 15)
        t_col = lax.broadcasted_iota(I32, (t_rows, _S_KEYS), 0)
        causal = idx <= PAST_LEN + t_col
        keep = ((keys > thr) | ((keys == thr) & (idx <= cut))) & causal
        mb = jnp.where(keep, 0.0, NEG)
        rep = jnp.broadcast_to(mb[:, None, :], (t_rows, A_HEADS, _S_KEYS)).reshape(t_rows * A_HEADS, _S_KEYS)
        mb_ref[0] = rep.astype(BF16)


def _dsa_index_sample(page_table, qi, wi, kidx_pool, ki_new):
    grid_spec = pltpu.PrefetchScalarGridSpec(
        num_scalar_prefetch=1,
        grid=(DEC_BATCH, N_PAGES),
        in_specs=[
            pl.BlockSpec((1, IDX_HEADS * DEC_SEQ, IDX_DIM), lambda b, p, pt: (b, 0, 0)),
            pl.BlockSpec((1, DEC_SEQ, IDX_HEADS), lambda b, p, pt: (b, 0, 0)),
            pl.BlockSpec((1, PAGE_SIZE, IDX_DIM), lambda b, p, pt: (pt[b * N_PAGES + p], 0, 0)),
            pl.BlockSpec((1, PAGE_SIZE, IDX_DIM), lambda b, p, pt: (b, 0, 0)),
        ],
        out_specs=pl.BlockSpec((1, DEC_SEQ * A_HEADS, _S_KEYS), lambda b, p, pt: (b, 0, 0)),
        scratch_shapes=[pltpu.VMEM((DEC_SEQ, _S_KEYS), I32)],
    )
    return pl.pallas_call(
        _dsa_index_sample_kernel,
        out_shape=jax.ShapeDtypeStruct((DEC_BATCH, DEC_SEQ * A_HEADS, _S_KEYS), BF16),
        grid_spec=grid_spec,
        compiler_params=_cparams(2),
        name="dsa_index_sample",
    )(page_table.reshape(-1), qi, wi, kidx_pool, ki_new)


def _paged_attn_finish(page, m_ref, l_ref, acc_ref, o_ref, q, knew_ref, vnew_ref, bias_new):
    @pl.when(page == N_PAGES - 1)
    def _():
        s = _dot_nt(q, knew_ref[0]) * ATTN_SCALE + bias_new
        m, l, acc = _softmax_step(s, vnew_ref[0], m_ref[...], l_ref[...], acc_ref[...])
        o_ref[0] = (acc / l).astype(o_ref.dtype)


def _dsa_attn_sample_kernel(pt_ref, q_ref, k_ref, v_ref, mb_ref, exp_ref, bias_ref, knew_ref, vnew_ref,
                            expnew_ref, biasnew_ref, o_ref, m_ref, l_ref, acc_ref):
    page = pl.program_id(1)
    rows = DEC_SEQ * A_HEADS

    @pl.when(page == 0)
    def _():
        m_ref[...] = jnp.full((rows, 1), NEG, F32)
        l_ref[...] = jnp.zeros((rows, 1), F32)
        acc_ref[...] = jnp.zeros((rows, HEAD_DIM), F32)

    q = q_ref[0]
    off = pl.multiple_of(page * PAGE_SIZE, PAGE_SIZE)
    sel = jnp.dot(mb_ref[0, :, pl.ds(off, PAGE_SIZE)], exp_ref[...], preferred_element_type=F32)
    s = _dot_nt(q, k_ref[0].astype(BF16)) * ATTN_SCALE + bias_ref[0] + sel
    m, l, acc = _softmax_step(s, v_ref[0].astype(BF16), m_ref[...], l_ref[...], acc_ref[...])
    m_ref[...] = m
    l_ref[...] = l
    acc_ref[...] = acc
    sel_new = jnp.dot(mb_ref[0, :, PAST_LEN:], expnew_ref[...], preferred_element_type=F32)
    _paged_attn_finish(page, m_ref, l_ref, acc_ref, o_ref, q, knew_ref, vnew_ref, biasnew_ref[...] + sel_new)


def _dsa_attn_sample(page_table, q, k_pool, v_pool, maskbias, expand, bias_pages, k_new, v_new, expand_new, bias_new):
    rows = DEC_SEQ * A_HEADS
    cols = PAGE_SIZE * A_HEADS
    grid_spec = pltpu.PrefetchScalarGridSpec(
        num_scalar_prefetch=1,
        grid=(DEC_BATCH, N_PAGES),
        in_specs=[
            pl.BlockSpec((1, rows, HEAD_DIM), lambda b, p, pt: (b, 0, 0)),
            pl.BlockSpec((1, cols, HEAD_DIM), lambda b, p, pt: (pt[b * N_PAGES + p], 0, 0)),
            pl.BlockSpec((1, cols, HEAD_DIM), lambda b, p, pt: (pt[b * N_PAGES + p], 0, 0)),
            pl.BlockSpec((1, rows, _S_KEYS), lambda b, p, pt: (b, 0, 0)),
            pl.BlockSpec((PAGE_SIZE, cols), lambda b, p, pt: (0, 0)),
            pl.BlockSpec((1, rows, cols), lambda b, p, pt: (jnp.where(p == N_PAGES - 1, 1, 0), 0, 0)),
            pl.BlockSpec((1, rows, HEAD_DIM), lambda b, p, pt: (b, 0, 0)),
            pl.BlockSpec((1, rows, HEAD_DIM), lambda b, p, pt: (b, 0, 0)),
            pl.BlockSpec((PAGE_SIZE, rows), lambda b, p, pt: (0, 0)),
            pl.BlockSpec((rows, rows), lambda b, p, pt: (0, 0)),
        ],
        out_specs=pl.BlockSpec((1, rows, HEAD_DIM), lambda b, p, pt: (b, 0, 0)),
        scratch_shapes=[pltpu.VMEM((rows, 1), F32), pltpu.VMEM((rows, 1), F32), pltpu.VMEM((rows, HEAD_DIM), F32)],
    )
    return pl.pallas_call(
        _dsa_attn_sample_kernel,
        out_shape=jax.ShapeDtypeStruct((DEC_BATCH, rows, HEAD_DIM), BF16),
        grid_spec=grid_spec,
        compiler_params=_cparams(2),
        name="dsa_attn_sample",
    )(page_table.reshape(-1), q, k_pool, v_pool, maskbias, expand, bias_pages, k_new, v_new, expand_new, bias_new)


def _split3(x):
    hi = x.astype(BF16)
    r = x - hi.astype(F32)
    mid = r.astype(BF16)
    lo = (r - mid.astype(F32)).astype(BF16)
    return hi, mid, lo


def _dot_hi(x, w):
    xh, xm, _ = _split3(x)
    wh, wm, _ = _split3(w)
    return (jnp.dot(xh, wh, preferred_element_type=F32) + jnp.dot(xh, wm, preferred_element_type=F32)
            + jnp.dot(xm, wh, preferred_element_type=F32))


def _cmul(c1, c2, x):
    return c1 * x + c2 * pltpu.roll(x, SSM_STATE, 1)


def _ssm_kernel(u_ref, m_ref, p_ref, q_ref, c1_ref, c2_ref, h0_ref, y_ref, hl_ref, *, n_seq, n_chunk):
    u = u_ref[0]
    s_all = _dot_hi(u, p_ref[0])
    c1 = c1_ref[0]
    c2 = c2_ref[0]
    prevs = []
    if n_chunk == 1:
        prevs.append(h0_ref[0])
        hl_ref[0] = s_all + _cmul(c1[0:1], c2[0:1], h0_ref[0])
    else:
        for b in range(n_seq):
            x = s_all[b * n_chunk:(b + 1) * n_chunk]
            row = lax.broadcasted_iota(I32, x.shape, 0)
            for lev in range(n_chunk.bit_length() - 1):
                d = 1 << lev
                shifted = jnp.where(row >= d, pltpu.roll(x, d, 0), 0.0)
                x = x + _cmul(c1[lev:lev + 1], c2[lev:lev + 1], shifted)
            prevs.append(jnp.where(row >= 1, pltpu.roll(x, 1, 0), 0.0))
            hl_ref[0, b:b + 1, :] = x[n_chunk - 1:n_chunk]
    h_prev_all = prevs[0] if len(prevs) == 1 else jnp.concatenate(prevs, axis=0)
    y = jnp.dot(u.astype(BF16), m_ref[0].astype(BF16), preferred_element_type=F32)
    y = y + jnp.dot(h_prev_all.astype(BF16), q_ref[0].astype(BF16), preferred_element_type=F32)
    y_ref[0] = y


def _ssm_tables(a_re, a_im, log_dt, b_re, b_im, c_re, c_im, d, chunk, n_lev):
    hp = lax.Precision.HIGHEST
    lam = lax.complex(a_re.astype(F32), a_im.astype(F32))
    dt = jnp.exp(log_dt.astype(F32))[:, None]
    a_bar = jnp.exp(lam * dt)
    bmat = lax.complex(b_re.astype(F32), b_im.astype(F32))
    b_bar = ((a_bar - 1.0) / lam)[..., None] * bmat
    cmat = lax.complex(c_re.astype(F32), c_im.astype(F32))
    pows = [jnp.ones_like(a_bar)]
    for _ in range(chunk):
        pows.append(pows[-1] * a_bar)
    apow = jnp.stack(pows, axis=0)
    ab = apow[:chunk, :, :, None] * b_bar[None]
    kern = jnp.einsum('gip,jgpk->gjik', cmat, ab, precision=hp).real
    g = SSM_GROUPS
    li = chunk * SSM_GROUP
    lag = jnp.arange(chunk)[None, :] - jnp.arange(chunk)[:, None]
    toe = kern[:, jnp.clip(lag, 0, chunk - 1)]
    toe = jnp.where((lag >= 0)[None, :, :, None, None], toe, 0.0)
    eye = jnp.eye(SSM_GROUP, dtype=F32)
    toe = toe + jnp.where((lag == 0)[None, :, :, None, None], d.astype(F32)[:, None, None, :, None] * eye, 0.0)
    m_tab = jnp.transpose(toe, (0, 1, 4, 2, 3)).reshape(g, li, li)
    abr = ab[::-1]
    p_c = jnp.transpose(abr, (1, 0, 3, 2))
    p_tab = jnp.concatenate([p_c.real, p_c.imag], axis=-1).reshape(g, li, 2 * SSM_STATE)
    ca = cmat[:, None, :, :] * jnp.transpose(apow[1:], (1, 0, 2))[:, :, None, :]
    q_c = jnp.transpose(ca, (0, 3, 1, 2)).reshape(g, SSM_STATE, li)
    q_tab = jnp.concatenate([q_c.real, -q_c.imag], axis=1)
    lev = [apow[chunk]]
    for _ in range(n_lev - 1):
        lev.append(lev[-1] * lev[-1])
    lev = jnp.stack(lev, axis=1)
    c1 = jnp.concatenate([lev.real, lev.real], axis=-1)
    c2 = jnp.concatenate([-lev.imag, lev.imag], axis=-1)
    return m_tab, p_tab, q_tab, c1, c2


def _ssm(u_g, tables, h0, n_seq, n_chunk, chunk):
    m_tab, p_tab, q_tab, c1, c2 = tables
    g = SSM_GROUPS
    r = n_seq * n_chunk
    li = chunk * SSM_GROUP
    n_lev = c1.shape[1]
    st = 2 * SSM_STATE
    spec3 = lambda a, b: pl.BlockSpec((1, a, b), lambda i: (i, 0, 0))
    return pl.pallas_call(
        functools.partial(_ssm_kernel, n_seq=n_seq, n_chunk=n_chunk),
        out_shape=[jax.ShapeDtypeStruct((g, r, li), F32), jax.ShapeDtypeStruct((g, n_seq, st), F32)],
        grid=(g,),
        in_specs=[spec3(r, li), spec3(li, li), spec3(li, st), spec3(st, li), spec3(n_lev, st), spec3(n_lev, st),
                  spec3(n_seq, st)],
        out_specs=[spec3(r, li), spec3(n_seq, st)],
        compiler_params=_cparams(1),
        name="ssm",
    )(u_g, m_tab, p_tab, q_tab, c1, c2, h0)


def _glu_kernel(y_ref, w_ref, b_ref, o_ref):
    y = y_ref[...]
    z = 0.5 * y * (1.0 + jnp.tanh(math.sqrt(2.0 / math.pi) * (y + 0.044715 * (y * y * y))))
    a = jnp.dot(z.astype(BF16), w_ref[...].astype(BF16), preferred_element_type=F32) + b_ref[...]
    o_ref[...] = (z * jax.nn.sigmoid(a)).astype(o_ref.dtype)


def _glu(y, w, b):
    return pl.pallas_call(
        _glu_kernel,
        out_shape=jax.ShapeDtypeStruct((N_ROWS, SSM_WIDTH), BF16),
        grid=(N_ROWS // MM_TILE,),
        in_specs=[
            pl.BlockSpec((MM_TILE, SSM_WIDTH), lambda i: (i, 0)),
            pl.BlockSpec((SSM_WIDTH, SSM_WIDTH), lambda i: (0, 0)),
            pl.BlockSpec((1, SSM_WIDTH), lambda i: (0, 0)),
        ],
        out_specs=pl.BlockSpec((MM_TILE, SSM_WIDTH), lambda i: (i, 0)),
        compiler_params=_cparams(1),
        name="glu",
    )(y, w, b.reshape(1, SSM_WIDTH))


def _cumsum_rows(x, tri, carry):
    h, m, lo = _split3(x)
    s = (jnp.dot(tri, h, preferred_element_type=F32) + jnp.dot(tri, m, preferred_element_type=F32)
         + jnp.dot(tri, lo, preferred_element_type=F32))
    return s + carry


def _logf_kernel(pf_ref, b_ref, lf_ref, cum_ref):
    x = pf_ref[...] + b_ref[...]
    lf = -(jnp.maximum(-x, 0.0) + jnp.log1p(jnp.exp(-jnp.abs(x))))
    lf_ref[...] = lf
    blk = 256
    r = lax.broadcasted_iota(I32, (blk, blk), 0)
    c = lax.broadcasted_iota(I32, (blk, blk), 1)
    tri = jnp.where(c <= r, 1.0, 0.0).astype(BF16)
    for b in range(BATCH):
        carry = jnp.zeros((1, C_HEADS), F32)
        for j in range(SEQ // blk):
            lo = b * SEQ + j * blk
            cs = _cumsum_rows(lf[lo:lo + blk], tri, carry)
            cum_ref[lo:lo + blk, :] = cs
            carry = cs[blk - 1:blk]


def _logf(pf, bias):
    return pl.pallas_call(
        _logf_kernel,
        out_shape=[jax.ShapeDtypeStruct((N_ROWS, C_HEADS), F32), jax.ShapeDtypeStruct((N_PROMPT, C_HEADS), F32)],
        grid=(1,),
        in_specs=[pl.BlockSpec((N_ROWS, C_HEADS), lambda i: (0, 0)), pl.BlockSpec((1, C_HEADS), lambda i: (0, 0))],
        out_specs=[pl.BlockSpec((N_ROWS, C_HEADS), lambda i: (0, 0)),
                   pl.BlockSpec((N_PROMPT, C_HEADS), lambda i: (0, 0))],
        compiler_params=_cparams(1),
        name="logf",
    )(pf, bias.reshape(1, C_HEADS))


def _fox_prompt_kernel(q_ref, k_ref, v_ref, nc_ref, o_ref):
    qb = pl.program_id(2)
    rows = Q_TILE
    q = q_ref[0]
    row_pos = qb * Q_TILE + lax.broadcasted_iota(I32, (rows, K_CHUNK), 0)
    col_iota = lax.broadcasted_iota(I32, (rows, K_CHUNK), 1)

    def logits(c):
        off = pl.multiple_of(c * K_CHUNK, K_CHUNK)
        s = _dot_nt(q, k_ref[0, pl.ds(off, K_CHUNK), :]) * ATTN_SCALE + nc_ref[0, 0, :, pl.ds(off, K_CHUNK)]
        return s, v_ref[0, pl.ds(off, K_CHUNK), :], off

    def body(c, carry):
        s, v, _ = logits(c)
        return _softmax_step(s, v, *carry)

    carry = (jnp.full((rows, 1), NEG, F32), jnp.zeros((rows, 1), F32), jnp.zeros((rows, HEAD_DIM), F32))
    carry = lax.fori_loop(0, qb, body, carry)
    s, v, off = logits(qb)
    s = jnp.where(off + col_iota <= row_pos, s, NEG)
    m, l, acc = _softmax_step(s, v, *carry)
    o_ref[0] = (acc / l).astype(o_ref.dtype)


def _fox_prompt(q, k, v, neg_cum):
    nq = SEQ // Q_TILE
    return pl.pallas_call(
        _fox_prompt_kernel,
        out_shape=jax.ShapeDtypeStruct((BATCH, SEQ, C_WIDTH), BF16),
        grid=(BATCH, C_HEADS, nq),
        in_specs=[
            pl.BlockSpec((1, Q_TILE, HEAD_DIM), lambda b, h, i: (b, i, h)),
            pl.BlockSpec((1, SEQ, HEAD_DIM), lambda b, h, i: (b, 0, h)),
            pl.BlockSpec((1, SEQ, HEAD_DIM), lambda b, h, i: (b, 0, h)),
            pl.BlockSpec((1, 1, 1, SEQ), lambda b, h, i: (b, h, 0, 0)),
        ],
        out_specs=pl.BlockSpec((1, Q_TILE, HEAD_DIM), lambda b, h, i: (b, i, h)),
        compiler_params=_cparams(3),
        name="fox_prompt",
    )(q, k, v, neg_cum)


_LF_ROWS = PAGE_SIZE * C_HEADS // 128


def _fox_sample_kernel(pt_ref, q_ref, k_ref, v_ref, lf_ref, hm_ref, sin_ref, sall_ref, knew_ref, vnew_ref,
                       lfnew_ref, hmnew_ref, o_ref, m_ref, l_ref, acc_ref, run_ref):
    page = pl.program_id(1)
    rows = DEC_SEQ * C_HEADS

    @pl.when(page == 0)
    def _():
        m_ref[...] = jnp.full((rows, 1), NEG, F32)
        l_ref[...] = jnp.zeros((rows, 1), F32)
        acc_ref[...] = jnp.zeros((rows, HEAD_DIM), F32)
        run_ref[...] = jnp.zeros((1, 128), F32)

    q = q_ref[0]
    sin = sin_ref[...]
    sall = sall_ref[...]
    x = lf_ref[0]
    xh, xm, xl = _split3(x)
    within = (jnp.dot(xh, sin, preferred_element_type=F32) + jnp.dot(xm, sin, preferred_element_type=F32)
              + jnp.dot(xl, sin, preferred_element_type=F32))
    tot = (jnp.dot(xh, sall, preferred_element_type=F32) + jnp.dot(xm, sall, preferred_element_type=F32)
           + jnp.dot(xl, sall, preferred_element_type=F32))
    r = lax.broadcasted_iota(I32, (_LF_ROWS, _LF_ROWS), 0)
    c = lax.broadcasted_iota(I32, (_LF_ROWS, _LF_ROWS), 1)
    strict = jnp.where(c < r, 1.0, 0.0).astype(BF16)
    th, tm, tl = _split3(tot)
    before = (jnp.dot(strict, th, preferred_element_type=F32) + jnp.dot(strict, tm, preferred_element_type=F32)
              + jnp.dot(strict, tl, preferred_element_type=F32))
    run = run_ref[...]
    cum = within + before + run
    run_new = run + before[_LF_ROWS - 1:_LF_ROWS] + tot[_LF_ROWS - 1:_LF_ROWS]
    run_ref[...] = run_new
    neg_cum = jnp.concatenate([-cum[j:j + 1] for j in range(_LF_ROWS)], axis=1)

    s = _dot_nt(q, k_ref[0].astype(BF16)) * ATTN_SCALE + hm_ref[...] + neg_cum
    m, l, acc = _softmax_step(s, v_ref[0].astype(BF16), m_ref[...], l_ref[...], acc_ref[...])
    m_ref[...] = m
    l_ref[...] = l
    acc_ref[...] = acc

    xn = lfnew_ref[0]
    nh, nm, nl = _split3(xn)
    cum_new = (jnp.dot(nh, sin, preferred_element_type=F32) + jnp.dot(nm, sin, preferred_element_type=F32)
               + jnp.dot(nl, sin, preferred_element_type=F32)) + run_new
    _paged_attn_finish(page, m_ref, l_ref, acc_ref, o_ref, q, knew_ref, vnew_ref, hmnew_ref[...] - cum_new)


def _fox_sample(page_table, q, k_pool, v_pool, lf_pool, head_mask, s_in, s_all, k_new, v_new, lf_new, head_mask_new):
    rows = DEC_SEQ * C_HEADS
    cols = PAGE_SIZE * C_HEADS
    grid_spec = pltpu.PrefetchScalarGridSpec(
        num_scalar_prefetch=1,
        grid=(DEC_BATCH, N_PAGES),
        in_specs=[
            pl.BlockSpec((1, rows, HEAD_DIM), lambda b, p, pt: (b, 0, 0)),
            pl.BlockSpec((1, cols, HEAD_DIM), lambda b, p, pt: (pt[b * N_PAGES + p], 0, 0)),
            pl.BlockSpec((1, cols, HEAD_DIM), lambda b, p, pt: (pt[b * N_PAGES + p], 0, 0)),
            pl.BlockSpec((1, _LF_ROWS, 128), lambda b, p, pt: (pt[b * N_PAGES + p], 0, 0)),
            pl.BlockSpec((rows, cols), lambda b, p, pt: (0, 0)),
            pl.BlockSpec((128, 128), lambda b, p, pt: (0, 0)),
            pl.BlockSpec((128, 128), lambda b, p, pt: (0, 0)),
            pl.BlockSpec((1, rows, HEAD_DIM), lambda b, p, pt: (b, 0, 0)),
            pl.BlockSpec((1, rows, HEAD_DIM), lambda b, p, pt: (b, 0, 0)),
            pl.BlockSpec((1, 1, 128), lambda b, p, pt: (b, 0, 0)),
            pl.BlockSpec((rows, rows), lambda b, p, pt: (0, 0)),
        ],
        out_specs=pl.BlockSpec((1, rows, HEAD_DIM), lambda b, p, pt: (b, 0, 0)),
        scratch_shapes=[pltpu.VMEM((rows, 1), F32), pltpu.VMEM((rows, 1), F32), pltpu.VMEM((rows, HEAD_DIM), F32),
                        pltpu.VMEM((1, 128), F32)],
    )
    return pl.pallas_call(
        _fox_sample_kernel,
        out_shape=jax.ShapeDtypeStruct((DEC_BATCH, rows, HEAD_DIM), BF16),
        grid_spec=grid_spec,
        compiler_params=_cparams(2),
        name="fox_sample",
    )(page_table.reshape(-1), q, k_pool, v_pool, lf_pool, head_mask, s_in, s_all, k_new, v_new, lf_new, head_mask_new)


def _router_kernel(h_ref, w_ref, b_ref, idx_ref, gate_ref):
    logits = jnp.dot(h_ref[...].astype(BF16), w_ref[...].astype(BF16), preferred_element_type=F32) + b_ref[...]
    iota = lax.broadcasted_iota(I32, logits.shape, 1)
    m1 = jnp.max(logits, axis=1, keepdims=True)
    i1 = jnp.min(jnp.where(logits == m1, iota, N_EXPERTS), axis=1, keepdims=True)
    rest = jnp.where(iota == i1, -jnp.inf, logits)
    m2 = jnp.max(rest, axis=1, keepdims=True)
    i2 = jnp.min(jnp.where(rest == m2, iota, N_EXPERTS), axis=1, keepdims=True)
    e2 = jnp.exp(m2 - m1)
    denom = 1.0 + e2
    col = lax.broadcasted_iota(I32, idx_ref.shape, 1)
    idx_ref[...] = jnp.where(col == 0, i1, i2)
    gate_ref[...] = jnp.where(col == 0, 1.0 / denom, e2 / denom)


def _router(h, w, b):
    return pl.pallas_call(
        _router_kernel,
        out_shape=[jax.ShapeDtypeStruct((N_ROWS, TOP_K), I32), jax.ShapeDtypeStruct((N_ROWS, TOP_K), F32)],
        grid=(N_ROWS // MM_TILE,),
        in_specs=[
            pl.BlockSpec((MM_TILE, D_MODEL), lambda i: (i, 0)),
            pl.BlockSpec((D_MODEL, N_EXPERTS), lambda i: (0, 0)),
            pl.BlockSpec((1, N_EXPERTS), lambda i: (0, 0)),
        ],
        out_specs=[pl.BlockSpec((MM_TILE, TOP_K), lambda i: (i, 0)), pl.BlockSpec((MM_TILE, TOP_K), lambda i: (i, 0))],
        compiler_params=_cparams(1),
        name="router",
    )(h, w, b.reshape(1, N_EXPERTS))


def _row_copy(src_hbm, dst_buf, sem, src_row, dst_row):
    return pltpu.make_async_copy(src_hbm.at[pl.ds(src_row, 1)], dst_buf.at[pl.ds(dst_row, 1)], sem)


def _gather_rows_kernel(tok_ref, x_hbm, o_ref, buf, sem):
    base = pl.program_id(0) * MOE_BLOCK

    def start(r, _):
        _row_copy(x_hbm, buf, sem, tok_ref[base + r], r).start()
        return 0

    def wait(r, _):
        _row_copy(x_hbm, buf, sem, 0, r).wait()
        return 0

    lax.fori_loop(0, MOE_BLOCK, start, 0)
    lax.fori_loop(0, MOE_BLOCK, wait, 0)
    o_ref[...] = buf[...].astype(o_ref.dtype)


def _gather_rows(row_tok, x):
    rows = row_tok.shape[0]
    grid_spec = pltpu.PrefetchScalarGridSpec(
        num_scalar_prefetch=1,
        grid=(rows // MOE_BLOCK,),
        in_specs=[pl.BlockSpec(memory_space=pl.ANY)],
        out_specs=pl.BlockSpec((MOE_BLOCK, D_MODEL), lambda i, tok: (i, 0)),
        scratch_shapes=[pltpu.VMEM((MOE_BLOCK, D_MODEL), F32), pltpu.SemaphoreType.DMA(())],
    )
    return pl.pallas_call(
        _gather_rows_kernel,
        out_shape=jax.ShapeDtypeStruct((rows, D_MODEL), BF16),
        grid_spec=grid_spec,
        compiler_params=_cparams(1),
        name="moe_gather",
    )(row_tok, x)


_COMBINE_TILE = 192


def _combine_kernel(pos_ref, y_hbm, g_ref, o_ref, buf0, buf1, sem0, sem1):
    base = pl.program_id(0) * _COMBINE_TILE

    def start(r, _):
        _row_copy(y_hbm, buf0, sem0, pos_ref[2 * (base + r)], r).start()
        _row_copy(y_hbm, buf1, sem1, pos_ref[2 * (base + r) + 1], r).start()
        return 0

    def wait(r, _):
        _row_copy(y_hbm, buf0, sem0, 0, r).wait()
        _row_copy(y_hbm, buf1, sem1, 0, r).wait()
        return 0

    lax.fori_loop(0, _COMBINE_TILE, start, 0)
    lax.fori_loop(0, _COMBINE_TILE, wait, 0)
    g = g_ref[...]
    o_ref[...] = buf0[...] * g[:, 0:1] + buf1[...] * g[:, 1:2]


def _combine(pos_flat, ys, gates):
    grid_spec = pltpu.PrefetchScalarGridSpec(
        num_scalar_prefetch=1,
        grid=(N_ROWS // _COMBINE_TILE,),
        in_specs=[pl.BlockSpec(memory_space=pl.ANY), pl.BlockSpec((_COMBINE_TILE, TOP_K), lambda i, pos: (i, 0))],
        out_specs=pl.BlockSpec((_COMBINE_TILE, D_MODEL), lambda i, pos: (i, 0)),
        scratch_shapes=[pltpu.VMEM((_COMBINE_TILE, D_MODEL), F32), pltpu.VMEM((_COMBINE_TILE, D_MODEL), F32),
                        pltpu.SemaphoreType.DMA(()), pltpu.SemaphoreType.DMA(())],
    )
    return pl.pallas_call(
        _combine_kernel,
        out_shape=jax.ShapeDtypeStruct((N_ROWS, D_MODEL), F32),
        grid_spec=grid_spec,
        compiler_params=_cparams(1),
        name="moe_combine",
    )(pos_flat, ys, gates)


def _moe(h, router_w, router_b, wg, wu, wd):
    idx, gates = _router(h, router_w, router_b)
    n_assign = N_ROWS * TOP_K
    n_blocks = (n_assign + N_EXPERTS * (MOE_BLOCK - 1) + MOE_BLOCK - 1) // MOE_BLOCK
    rows = n_blocks * MOE_BLOCK
    e_flat = idx.reshape(-1)
    onehot = (e_flat[:, None] == jnp.arange(N_EXPERTS, dtype=I32)[None, :]).astype(I32)
    csum = jnp.cumsum(onehot, axis=0)
    rank = jnp.sum(csum * onehot, axis=1) - 1
    counts = csum[-1]
    padded = (counts + MOE_BLOCK - 1) // MOE_BLOCK * MOE_BLOCK
    pad_end = jnp.cumsum(padded)
    pad_start = pad_end - padded
    dest = (jnp.sum(pad_start[None, :] * onehot, axis=1) + rank).astype(I32)
    tok_flat = jnp.repeat(jnp.arange(N_ROWS, dtype=I32), TOP_K)
    row_tok = jnp.zeros((rows,), I32).at[dest].set(tok_flat)
    blk_start = jnp.arange(n_blocks, dtype=I32) * MOE_BLOCK
    blk_exp = jnp.minimum(jnp.sum((blk_start[:, None] >= pad_end[None, :]).astype(I32), axis=1), N_EXPERTS - 1)

    xs = _gather_rows(row_tok, h)
    hmid = _grouped_mm(xs, [wg, wu], blk_exp, MOE_BLOCK, 1024, BF16)
    ys = _grouped_mm(hmid, [wd], blk_exp, MOE_BLOCK, 256, F32)
    return _combine(dest, ys, gates)


def _t5_bucket(dist):
    n = jnp.maximum(dist, 0)
    exact = REL_BUCKETS // 2
    nf = jnp.maximum(n, 1).astype(F32)
    large = exact + (jnp.log(nf / exact) / math.log(REL_MAX_DIST / exact) * (REL_BUCKETS - exact)).astype(I32)
    large = jnp.minimum(large, REL_BUCKETS - 1)
    return jnp.where(n < exact, n, large)


def _bias_of_dist(rel_bias, dist):
    b = jnp.moveaxis(rel_bias[_t5_bucket(dist)], -1, 0).astype(F32)
    return jnp.where((dist >= 0)[None], b, NEG)


def _layer_even(h, page_table, cache_a_k, cache_a_v, cache_a_kidx, state_re, state_im, rel_bias, w_in, ssm_p,
                glu_w, glu_b, w_out):
    qkvi = _dense_mm(h, w_in[:, :4 * A_WIDTH], 1024)
    c0 = 4 * A_WIDTH
    kw = _dense_mm(h, w_in[:, c0:c0 + IDX_DIM + IDX_HEADS], IDX_DIM + IDX_HEADS)
    u = _dense_mm(h, w_in[:, c0 + IDX_DIM + IDX_HEADS:], 1024)
    q, k, v, qi = (qkvi[:, i * A_WIDTH:(i + 1) * A_WIDTH] for i in range(4))
    ki = kw[:, :IDX_DIM]
    wi = kw[:, IDX_DIM:]

    def prompt(a):
        return a[:N_PROMPT].reshape(BATCH, SEQ, a.shape[-1])

    def sample(a):
        return a[N_PROMPT:].reshape(DEC_BATCH, DEC_SEQ, a.shape[-1])

    kit = jnp.swapaxes(prompt(ki), 1, 2).astype(BF16)
    maskbias = _dsa_index_prompt(prompt(qi).astype(BF16), prompt(wi), kit)
    ii = jnp.arange(Q_TILE, dtype=I32)[:, None]
    jj = jnp.arange(K_CHUNK, dtype=I32)[None, :]
    tile_bias = jnp.stack([_bias_of_dist(rel_bias, ii - jj), _bias_of_dist(rel_bias, ii - jj + K_CHUNK)], axis=0)
    far_bias = rel_bias[REL_BUCKETS - 1].astype(F32)
    att_p = _dsa_attn_prompt(far_bias, prompt(q).astype(BF16), prompt(k).astype(BF16), prompt(v).astype(BF16),
                             maskbias, tile_bias)

    qi_s = sample(qi).reshape(DEC_BATCH, DEC_SEQ, IDX_HEADS, IDX_DIM)
    qi_s = jnp.swapaxes(qi_s, 1, 2).reshape(DEC_BATCH, IDX_HEADS * DEC_SEQ, IDX_DIM).astype(BF16)
    ki_new = jnp.pad(sample(ki), ((0, 0), (0, PAGE_SIZE - DEC_SEQ), (0, 0))).astype(BF16)
    mb_s = _dsa_index_sample(page_table, qi_s, sample(wi), cache_a_kidx, ki_new)
    rows = DEC_SEQ * A_HEADS
    cols = PAGE_SIZE * A_HEADS
    row_t = jnp.arange(rows, dtype=I32) // A_HEADS
    row_h = jnp.arange(rows, dtype=I32) % A_HEADS
    col_p = jnp.arange(cols, dtype=I32) // A_HEADS
    col_h = jnp.arange(cols, dtype=I32) % A_HEADS
    head_ok = row_h[:, None] == col_h[None, :]
    far_rows = rel_bias[REL_BUCKETS - 1].astype(F32)[row_h][:, None]
    dist_last = (PAGE_SIZE + row_t)[:, None] - col_p[None, :]
    near_rows = rel_bias[_t5_bucket(dist_last), row_h[:, None]].astype(F32)
    bias_pages = jnp.stack([jnp.where(head_ok, jnp.broadcast_to(far_rows, (rows, cols)), NEG),
                            jnp.where(head_ok, near_rows, NEG)], axis=0)
    new_s = jnp.arange(rows, dtype=I32) // A_HEADS
    dist_new = row_t[:, None] - new_s[None, :]
    ok_new = (row_h[:, None] == row_h[None, :]) & (dist_new >= 0)
    bias_new = jnp.where(ok_new, rel_bias[_t5_bucket(dist_new), row_h[:, None]].astype(F32), NEG)
    expand = (jnp.arange(PAGE_SIZE, dtype=I32)[:, None] == col_p[None, :]).astype(BF16)
    expand_new = (jnp.arange(PAGE_SIZE, dtype=I32)[:, None] == new_s[None, :]).astype(BF16)
    att_s = _dsa_attn_sample(
        page_table, sample(q).reshape(DEC_BATCH, rows, HEAD_DIM).astype(BF16),
        cache_a_k.reshape(-1, cols, HEAD_DIM), cache_a_v.reshape(-1, cols, HEAD_DIM), mb_s, expand, bias_pages,
        sample(k).reshape(DEC_BATCH, rows, HEAD_DIM).astype(BF16),
        sample(v).reshape(DEC_BATCH, rows, HEAD_DIM).astype(BF16), expand_new, bias_new)
    att = jnp.concatenate([att_p.reshape(N_PROMPT, A_WIDTH), att_s.reshape(N_SAMPLE, A_WIDTH)], axis=0)

    n_chunk = SEQ // SSM_CHUNK
    tab_p = _ssm_tables(*ssm_p, chunk=SSM_CHUNK, n_lev=8)
    tab_s = _ssm_tables(*ssm_p, chunk=DEC_SEQ, n_lev=8)
    u_p = u[:N_PROMPT].reshape(BATCH, n_chunk, SSM_CHUNK, SSM_GROUPS, SSM_GROUP)
    u_p = jnp.transpose(u_p, (3, 0, 1, 2, 4)).reshape(SSM_GROUPS, BATCH * n_chunk, SSM_CHUNK * SSM_GROUP)
    zero_state = jnp.zeros((SSM_GROUPS, BATCH, 2 * SSM_STATE), F32)
    y_p, hl_p = _ssm(u_p, tab_p, zero_state, BATCH, n_chunk, SSM_CHUNK)
    y_p = y_p.reshape(SSM_GROUPS, BATCH, n_chunk, SSM_CHUNK, SSM_GROUP)
    y_p = jnp.transpose(y_p, (1, 2, 3, 0, 4)).reshape(N_PROMPT, SSM_WIDTH)
    u_s = u[N_PROMPT:].reshape(DEC_BATCH, DEC_SEQ, SSM_GROUPS, SSM_GROUP)
    u_s = jnp.transpose(u_s, (2, 0, 1, 3)).reshape(SSM_GROUPS, DEC_BATCH, DEC_SEQ * SSM_GROUP)
    h0 = jnp.concatenate([state_re.astype(F32), state_im.astype(F32)], axis=-1)
    y_s, hl_s = _ssm(u_s, tab_s, jnp.swapaxes(h0, 0, 1), DEC_BATCH, 1, DEC_SEQ)
    y_s = y_s.reshape(SSM_GROUPS, DEC_BATCH, DEC_SEQ, SSM_GROUP)
    y_s = jnp.transpose(y_s, (1, 2, 0, 3)).reshape(N_SAMPLE, SSM_WIDTH)
    ssm = _glu(jnp.concatenate([y_p, y_s], axis=0), glu_w, glu_b)

    mixed = jnp.concatenate([att, ssm], axis=-1)
    y = _dense_mm(mixed, w_out, 1024)
    hl_p = jnp.swapaxes(hl_p, 0, 1)
    hl_s = jnp.swapaxes(hl_s, 0, 1)
    state = dict(
        k_p=prompt(k).reshape(1, BATCH, SEQ, A_HEADS, HEAD_DIM), k_s=sample(k).reshape(1, DEC_BATCH, DEC_SEQ, A_HEADS, HEAD_DIM),
        v_p=prompt(v).reshape(1, BATCH, SEQ, A_HEADS, HEAD_DIM), v_s=sample(v).reshape(1, DEC_BATCH, DEC_SEQ, A_HEADS, HEAD_DIM),
        ki_p=prompt(ki)[None], ki_s=sample(ki)[None],
        re_p=hl_p[None, :, :, :SSM_STATE], re_s=hl_s[None, :, :, :SSM_STATE],
        im_p=hl_p[None, :, :, SSM_STATE:], im_s=hl_s[None, :, :, SSM_STATE:])
    return y, state


def _layer_odd(h, page_table, cache_c_k, cache_c_v, cache_c_logf, w_in, fox_b, w_out):
    qkv = _dense_mm(h, w_in[:, :3 * C_WIDTH], 1024)
    pf = _dense_mm(h, w_in[:, 3 * C_WIDTH:], C_HEADS)
    q, k, v = (qkv[:, i * C_WIDTH:(i + 1) * C_WIDTH] for i in range(3))
    logf, cum_p = _logf(pf, fox_b)

    def prompt(a):
        return a[:N_PROMPT].reshape(BATCH, SEQ, a.shape[-1])

    def sample(a):
        return a[N_PROMPT:].reshape(DEC_BATCH, DEC_SEQ, a.shape[-1])

    neg_cum = -jnp.swapaxes(cum_p.reshape(BATCH, SEQ, C_HEADS), 1, 2).reshape(BATCH, C_HEADS, 1, SEQ)
    out_p = _fox_prompt(prompt(q).astype(BF16), prompt(k).astype(BF16), prompt(v).astype(BF16), neg_cum)

    rows = DEC_SEQ * C_HEADS
    cols = PAGE_SIZE * C_HEADS
    row_t = jnp.arange(rows, dtype=I32) // C_HEADS
    row_h = jnp.arange(rows, dtype=I32) % C_HEADS
    col_h = jnp.arange(cols, dtype=I32) % C_HEADS
    head_mask = jnp.where(row_h[:, None] == col_h[None, :], 0.0, NEG).astype(F32)
    ok_new = (row_h[:, None] == row_h[None, :]) & (row_t[None, :] <= row_t[:, None])
    head_mask_new = jnp.where(ok_new, 0.0, NEG).astype(F32)
    lane_p = jnp.arange(128, dtype=I32) // C_HEADS
    lane_h = jnp.arange(128, dtype=I32) % C_HEADS
    same_head = lane_h[:, None] == lane_h[None, :]
    s_in = (same_head & (lane_p[:, None] <= lane_p[None, :])).astype(BF16)
    s_all = same_head.astype(BF16)
    out_s = _fox_sample(
        page_table, sample(q).reshape(DEC_BATCH, rows, HEAD_DIM).astype(BF16),
        cache_c_k.reshape(-1, cols, HEAD_DIM), cache_c_v.reshape(-1, cols, HEAD_DIM),
        cache_c_logf.astype(F32).reshape(-1, _LF_ROWS, 128), head_mask, s_in, s_all,
        sample(k).reshape(DEC_BATCH, rows, HEAD_DIM).astype(BF16),
        sample(v).reshape(DEC_BATCH, rows, HEAD_DIM).astype(BF16),
        logf[N_PROMPT:].reshape(DEC_BATCH, 1, DEC_SEQ * C_HEADS), head_mask_new)
    att = jnp.concatenate([out_p.reshape(N_PROMPT, C_WIDTH), out_s.reshape(N_SAMPLE, C_WIDTH)], axis=0)
    y = _dense_mm(att, w_out, 1024)
    state = dict(
        k_p=prompt(k).reshape(1, BATCH, SEQ, C_HEADS, HEAD_DIM), k_s=sample(k).reshape(1, DEC_BATCH, DEC_SEQ, C_HEADS, HEAD_DIM),
        v_p=prompt(v).reshape(1, BATCH, SEQ, C_HEADS, HEAD_DIM), v_s=sample(v).reshape(1, DEC_BATCH, DEC_SEQ, C_HEADS, HEAD_DIM),
        lf_p=prompt(logf)[None], lf_s=sample(logf)[None])
    return y, state


def kernel(x_prompt, x_sample, cache_a_k, cache_a_v, cache_a_kidx, state_ssm_re, state_ssm_im, cache_c_k, cache_c_v, cache_c_logf, page_table, c_prompt, c_sample, rel_bias, ada_w, ada_b, ln_g, ln_b, w_in_e, ssm_a_re, ssm_a_im, ssm_log_dt, ssm_b_re, ssm_b_im, ssm_c_re, ssm_c_im, ssm_d, glu_w, glu_b, w_out_e, ffn_w_gate, ffn_w_up, ffn_w_down, w_in_o, fox_b_f, w_out_o, router_w, router_b, moe_w_gate, moe_w_up, moe_w_down):
    x = jnp.concatenate([x_prompt.reshape(N_PROMPT, D_MODEL), x_sample.reshape(N_SAMPLE, D_MODEL)], axis=0)
    c_all = jnp.concatenate([c_prompt, c_sample, jnp.zeros((N_COND - BATCH - DEC_BATCH, D_MODEL), F32)], axis=0)
    mod = _ada_mod(c_all, ada_w, ada_b).reshape(DEPTH, N_COND, 6, D_MODEL)
    m0 = [mod[0, :, i] for i in range(6)]
    m1 = [mod[1, :, i] for i in range(6)]

    h = _modulate(x, m0[0], m0[1])
    ssm_p = (ssm_a_re[0], ssm_a_im[0], ssm_log_dt[0], ssm_b_re[0], ssm_b_im[0], ssm_c_re[0], ssm_c_im[0], ssm_d[0])
    y, st_e = _layer_even(h, page_table, cache_a_k[0], cache_a_v[0], cache_a_kidx[0], state_ssm_re[0],
                          state_ssm_im[0], rel_bias, w_in_e[0], ssm_p, glu_w[0], glu_b[0], w_out_e[0])
    x, h = _ln_mod(x, y, m0[2], ln_g[0, 0], ln_b[0, 0], m0[3], m0[4])
    hmid = _dense_mm_gated(h, ffn_w_gate[0], ffn_w_up[0], 512)
    y = _dense_mm(hmid, ffn_w_down[0], 512)
    x, h = _ln_mod(x, y, m0[5], ln_g[0, 1], ln_b[0, 1], m1[0], m1[1])

    y, st_o = _layer_odd(h, page_table, cache_c_k[0], cache_c_v[0], cache_c_logf[0], w_in_o[0], fox_b_f[0],
                         w_out_o[0])
    x, h = _ln_mod(x, y, m1[2], ln_g[1, 0], ln_b[1, 0], m1[3], m1[4], h_dtype=F32)
    y = _moe(h, router_w[0], router_b[0], moe_w_gate[0], moe_w_up[0], moe_w_down[0])
    x, _ = _ln_mod(x, y, m1[5], ln_g[1, 1], ln_b[1, 1])

    y_prompt = x[:N_PROMPT].reshape(BATCH, SEQ, D_MODEL)
    y_sample = x[N_PROMPT:].reshape(DEC_BATCH, DEC_SEQ, D_MODEL)
    return (y_prompt, y_sample, st_e['k_p'], st_e['k_s'], st_e['v_p'], st_e['v_s'], st_e['ki_p'], st_e['ki_s'],
            st_e['re_p'], st_e['re_s'], st_e['im_p'], st_e['im_s'], st_o['k_p'], st_o['k_s'], st_o['v_p'],
            st_o['v_s'], st_o['lf_p'], st_o['lf_s'])
```
